```python
import jax, jax.numpy as jnp
from jax import lax
import numpy as np

D_MODEL = 2048
BATCH = 4
SEQ = 2048
DEPTH = 1
DEC_BATCH = 128
DEC_SEQ = 4
PAST_LEN = 16384
PAGE_SIZE = 128

MIX_WIDTH = D_MODEL
W_A = MIX_WIDTH // 2
W_B = MIX_WIDTH - W_A
CHUNK = 128
H_A = 8
DA = W_A // H_A
HEAD_B = 64
H_B = W_B // HEAD_B
LORA_W = 64
LORA_A = 64
LORA_G = 160
W_RWKV_IN = 3 * W_B + LORA_W + LORA_A + LORA_G
W_IN = 2 * W_A + W_RWKV_IN
N_KEYS = 128
N_EXPERTS = N_KEYS * N_KEYS
PEER_HEADS = 8
D_KEY = 256
HALF_KEY = D_KEY // 2
TOPK = 16
TOK_BLOCK = 128
RMS_EPS = 1e-6
LN_EPS = 1e-5
GN_EPS = 6.4e-4

kernel_name = 'hymba_gmlp_rwkv7_peer_step'


def rms_norm(x, g):
    xf = x.astype(jnp.float32)
    y = xf * lax.rsqrt(jnp.mean(xf * xf, axis=-1, keepdims=True) + RMS_EPS)
    return (y * g.astype(jnp.float32)).astype(x.dtype)


def layer_norm(x, g, b, eps):
    xf = x.astype(jnp.float32)
    mu = jnp.mean(xf, axis=-1, keepdims=True)
    xc = xf - mu
    var = jnp.mean(xc * xc, axis=-1, keepdims=True)
    return xc * lax.rsqrt(var + eps) * g.astype(jnp.float32) + b.astype(jnp.float32)


def chunk_spatial_gate(z, ln_g, ln_b, ws, bs):
    bn, t_len, _ = z.shape
    u, v = z[..., :W_A], z[..., W_A:]
    v = layer_norm(v, ln_g, ln_b, LN_EPS).astype(z.dtype)
    n_chunks = -(-t_len // CHUNK)
    pad = n_chunks * CHUNK - t_len
    vp = jnp.pad(v, ((0, 0), (0, pad), (0, 0))).reshape(bn, n_chunks, CHUNK, H_A, DA)
    mask = jnp.tril(jnp.ones((CHUNK, CHUNK), dtype=bool))
    ws_c = jnp.where(mask[None], ws, 0)
    mixed = jnp.einsum('hts,bcshd->bcthd', ws_c, vp) + bs.T[None, None, :, :, None]
    mixed = mixed.reshape(bn, n_chunks * CHUNK, W_A)[:, :t_len]
    start = ((t_len - 1) // CHUNK) * CHUNK
    return u * mixed, v[:, start:]


def rwkv7_mix(p, p_prev_row, s0, mu, w0, w_up, a0, a_up, g_up, k_k, k_a, r_k, gn_g, gn_b):
    f32 = jnp.float32
    bn, t_len, _ = p.shape
    pf = p.astype(f32)
    p_prev = jnp.concatenate([p_prev_row.astype(f32)[:, None], pf[:, :-1]], axis=1)
    q = pf + (p_prev - pf) * mu.astype(f32)
    r = q[..., :W_B]
    k = q[..., W_B:2 * W_B]
    v = q[..., 2 * W_B:3 * W_B]
    o = 3 * W_B
    xw = q[..., o:o + LORA_W]
    xa = q[..., o + LORA_W:o + LORA_W + LORA_A]
    xg = q[..., o + LORA_W + LORA_A:]
    w = -jax.nn.softplus(-(w0.astype(f32) + jnp.tanh(xw) @ w_up.astype(f32))) - 0.5
    decay = jnp.exp(-jnp.exp(w))
    a = jax.nn.sigmoid(a0.astype(f32) + xa @ a_up.astype(f32))
    g = jax.nn.sigmoid(xg) @ g_up.astype(f32)
    hs = (bn, t_len, H_B, HEAD_B)
    kk = (k * k_k.astype(f32)).reshape(hs)
    kk = kk * lax.rsqrt(jnp.maximum(jnp.sum(kk * kk, axis=-1, keepdims=True), 1e-24))
    k = k * (1.0 + (a - 1.0) * k_a.astype(f32))
    r_h, k_h, v_h = r.reshape(hs), k.reshape(hs), v.reshape(hs)
    a_h, d_h = a.reshape(hs), decay.reshape(hs)

    def step(state, inp):
        r_t, w_t, k_t, v_t, a_t, b_t = inp
        sa = jnp.einsum('bhij,bhj->bhi', state, a_t)
        state = (state * w_t[:, :, None, :] + sa[..., None] * b_t[:, :, None, :]
                 + v_t[..., None] * k_t[:, :, None, :])
        y_t = jnp.einsum('bhij,bhj->bhi', state, r_t)
        return state, y_t

    xs = tuple(jnp.moveaxis(t, 1, 0) for t in (r_h, d_h, k_h, v_h, -kk, kk * a_h))
    s_final, ys = lax.scan(step, s0.astype(f32), xs)
    y = jnp.moveaxis(ys, 0, 1)
    y = layer_norm(y, gn_g.reshape(H_B, HEAD_B), gn_b.reshape(H_B, HEAD_B), GN_EPS)
    bonus = jnp.sum(r_h * k_h * r_k.astype(f32), axis=-1, keepdims=True) * v_h
    out = (y + bonus).reshape(bn, t_len, W_B) * g
    return out.astype(p.dtype), s_final


def peer_ffn(xn, w_q, sub_keys, u_tab, v_tab):
    f32 = jnp.float32
    bn, t_len, d = xn.shape
    n_tok = bn * t_len
    xt = xn.reshape(n_tok, d)
    q = (xt @ w_q).reshape(n_tok, PEER_HEADS, 2, HALF_KEY).astype(f32)
    s = jnp.einsum('nhcd,ckd->nhck', q, sub_keys.astype(f32))
    top_s, top_i = lax.top_k(s, TOPK)
    cand_s = top_s[..., 0, :, None] + top_s[..., 1, None, :]
    cand_i = top_i[..., 0, :, None] * N_KEYS + top_i[..., 1, None, :]
    best_s, best_j = lax.top_k(cand_s.reshape(n_tok, PEER_HEADS, TOPK * TOPK), TOPK)
    experts = jnp.take_along_axis(cand_i.reshape(n_tok, PEER_HEADS, TOPK * TOPK), best_j, axis=-1)
    gates = jax.nn.softmax(best_s, axis=-1)
    n_blk = -(-n_tok // TOK_BLOCK)
    pad = n_blk * TOK_BLOCK - n_tok
    xb = jnp.pad(xt, ((0, pad), (0, 0))).reshape(n_blk, TOK_BLOCK, d)
    eb = jnp.pad(experts, ((0, pad), (0, 0), (0, 0))).reshape(n_blk, TOK_BLOCK, PEER_HEADS, TOPK)
    gb = jnp.pad(gates, ((0, pad), (0, 0), (0, 0))).reshape(n_blk, TOK_BLOCK, PEER_HEADS, TOPK)

    def block(args):
        x_b, e_b, g_b = args
        ue = u_tab[e_b]
        act = jax.nn.gelu(jnp.einsum('nhkd,nd->nhk', ue, x_b).astype(f32), approximate=False)
        coef = (g_b * act).astype(x_b.dtype)
        return jnp.einsum('nhk,nhkd->nd', coef, v_tab[e_b])

    out = lax.map(block, (xb, eb, gb)).reshape(n_blk * TOK_BLOCK, d)[:n_tok]
    return out.reshape(bn, t_len, d)


def _layer(x, shift_row, s0, norm1_g, w_in, ln_v_g, ln_v_b, ws, bs, mu, w0, w_up, a0, a_up,
           g_up, k_k, k_a, r_k, gn_g, gn_b, w_out, norm2_g, w_q, sub_keys, u_tab, v_tab):
    xn = rms_norm(x, norm1_g)
    proj = xn @ w_in
    z_a = jax.nn.gelu(proj[..., :2 * W_A], approximate=False)
    p_b = proj[..., 2 * W_A:]
    prev_row = shift_row.astype(xn.dtype) @ w_in[:, 2 * W_A:]
    y_a, v_rows = chunk_spatial_gate(z_a, ln_v_g, ln_v_b, ws, bs)
    y_b, s_new = rwkv7_mix(p_b, prev_row, s0, mu, w0, w_up, a0, a_up, g_up, k_k, k_a, r_k, gn_g, gn_b)
    h = x + jnp.concatenate([y_a, y_b], axis=-1) @ w_out
    y = h + peer_ffn(rms_norm(h, norm2_g), w_q, sub_keys, u_tab, v_tab)
    return y, s_new, xn[:, -1], v_rows


def setup_inputs(seed: int = 0) -> dict:
    key = jax.random.key(seed)
    ks = jax.random.split(key, 28)
    f32 = jnp.float32

    def nrm(k, shape, s):
        return s * jax.random.normal(k, shape, f32)

    L = DEPTH
    return {
        'x_prompt': nrm(ks[0], (BATCH, SEQ, D_MODEL), 1.0),
        'x_sample': nrm(ks[1], (DEC_BATCH, DEC_SEQ, D_MODEL), 1.0),
        'state_wkv': nrm(ks[2], (L, DEC_BATCH, H_B, HEAD_B, HEAD_B), 0.3),
        'state_shift': nrm(ks[3], (L, DEC_BATCH, D_MODEL), 1.0),
        'norm1_g': 1.0 + nrm(ks[4], (L, D_MODEL), 0.02),
        'w_in': nrm(ks[5], (L, D_MODEL, W_IN), D_MODEL ** -0.5),
        'ln_v_g': 1.0 + nrm(ks[6], (L, W_A), 0.02),
        'ln_v_b': nrm(ks[7], (L, W_A), 0.02),
        'ws': nrm(ks[8], (L, H_A, CHUNK, CHUNK), 0.5 * CHUNK ** -0.5),
        'bs': 1.0 + nrm(ks[9], (L, H_A, CHUNK), 0.1),
        'mu': jax.random.uniform(ks[10], (L, W_RWKV_IN), f32),
        'w0': -1.0 + nrm(ks[11], (L, W_B), 0.5),
        'w_up': nrm(ks[12], (L, LORA_W, W_B), 0.1),
        'a0': nrm(ks[13], (L, W_B), 0.1),
        'a_up': nrm(ks[14], (L, LORA_A, W_B), 0.1),
        'g_up': nrm(ks[15], (L, LORA_G, W_B), LORA_G ** -0.5),
        'k_k': 0.85 + nrm(ks[16], (L, W_B), 0.05),
        'k_a': 1.0 + nrm(ks[17], (L, W_B), 0.05),
        'r_k': nrm(ks[18], (L, H_B, HEAD_B), 0.1),
        'gn_g': 1.0 + nrm(ks[19], (L, W_B), 0.02),
        'gn_b': nrm(ks[20], (L, W_B), 0.02),
        'w_out': nrm(ks[21], (L, W_A + W_B, D_MODEL), (W_A + W_B) ** -0.5),
        'norm2_g': 1.0 + nrm(ks[22], (L, D_MODEL), 0.02),
        'w_q': nrm(ks[23], (L, D_MODEL, PEER_HEADS * D_KEY), D_MODEL ** -0.5),
        'sub_keys': nrm(ks[24], (L, 2, N_KEYS, HALF_KEY), HALF_KEY ** -0.5),
        'u_tab': nrm(ks[25], (L, N_EXPERTS, D_MODEL), D_MODEL ** -0.5),
        'v_tab': nrm(ks[26], (L, N_EXPERTS, D_MODEL), 0.1),
        'final_g': 1.0 + nrm(ks[27], (D_MODEL,), 0.02),
    }


def reference(x_prompt, x_sample, state_wkv, state_shift, norm1_g, w_in, ln_v_g, ln_v_b, ws, bs,
              mu, w0, w_up, a0, a_up, g_up, k_k, k_a, r_k, gn_g, gn_b, w_out, norm2_g, w_q,
              sub_keys, u_tab, v_tab, final_g):
    bp = x_prompt.shape[0]
    shift0 = jnp.zeros((bp, D_MODEL), x_prompt.dtype)
    wkv0 = jnp.zeros((bp, H_B, HEAD_B, HEAD_B), jnp.float32)
    yp, ys = x_prompt, x_sample
    p_wkv, p_shift, p_chunk, s_wkv, s_shift, s_chunk = [], [], [], [], [], []
    for l in range(DEPTH):
        lw = [w[l] for w in (norm1_g, w_in, ln_v_g, ln_v_b, ws, bs, mu, w0, w_up, a0, a_up, g_up,
                             k_k, k_a, r_k, gn_g, gn_b, w_out, norm2_g, w_q, sub_keys, u_tab, v_tab)]
        yp, wkv_p, sh_p, cv_p = _layer(yp, shift0, wkv0, *lw)
        ys, wkv_s, sh_s, cv_s = _layer(ys, state_shift[l], state_wkv[l], *lw)
        p_wkv.append(wkv_p); p_shift.append(sh_p); p_chunk.append(cv_p)
        s_wkv.append(wkv_s); s_shift.append(sh_s); s_chunk.append(cv_s)
    y_prompt = rms_norm(yp, final_g)
    y_sample = rms_norm(ys, final_g)
    return (y_prompt, y_sample, jnp.stack(p_wkv), jnp.stack(p_shift), jnp.stack(p_chunk),
            jnp.stack(s_wkv), jnp.stack(s_shift), jnp.stack(s_chunk))
```

```python
import functools
import math

import jax
import jax.numpy as jnp
from jax import lax
from jax.experimental import pallas as pl
from jax.experimental.pallas import tpu as pltpu

f32 = jnp.float32
bf16 = jnp.bfloat16

D_MODEL = 2048
W_A = 1024
W_B = 1024
H_A = 8
DA = 128
CHUNK_A = 128
HEAD_B = 64
H_B = 16
LORA_W = 64
LORA_A = 64
LORA_G = 160
W_RWKV_IN = 3 * W_B + LORA_W + LORA_A + LORA_G
W_RWKV_PAD = 4 * W_B
W_CAT = W_RWKV_PAD + 2 * W_A
N_KEYS = 128
PEER_HEADS = 8
HALF_KEY = 128
TOPK = 16
RMS_EPS = 1e-6
LN_EPS = 1e-5
GN_EPS = 6.4e-4
SAMPLE_ROWS = 8
VMEM_LIMIT = 56 * 1024 * 1024
HI = lax.Precision.HIGHEST
INV_SQRT2 = 1.0 / math.sqrt(2.0)
NEG_INF = float("-inf")


def _params(*sem):
    return pltpu.CompilerParams(dimension_semantics=sem, vmem_limit_bytes=VMEM_LIMIT)


def _gelu(x):
    return 0.5 * x * (1.0 + lax.erf(x * INV_SQRT2))


def _dot(a, b, prec=None):
    return jnp.dot(a, b, preferred_element_type=f32, precision=prec)


def _dot_nt(a, b, prec=None):
    return lax.dot_general(a, b, (((1,), (1,)), ((), ())), preferred_element_type=f32, precision=prec)


def _dot_tn(a, b, prec=None):
    return lax.dot_general(a, b, (((0,), (0,)), ((), ())), preferred_element_type=f32, precision=prec)


def _rmsnorm_kernel(x_ref, g_ref, o_ref):
    x = x_ref[...]
    y = x * lax.rsqrt(jnp.mean(x * x, axis=-1, keepdims=True) + RMS_EPS)
    o_ref[...] = (y * g_ref[...]).astype(o_ref.dtype)


def _rmsnorm(x, g, out_dtype, tm):
    n, d = x.shape
    return pl.pallas_call(
        _rmsnorm_kernel,
        grid=(n // tm,),
        in_specs=[pl.BlockSpec((tm, d), lambda i: (i, 0)), pl.BlockSpec((1, d), lambda i: (0, 0))],
        out_specs=pl.BlockSpec((tm, d), lambda i: (i, 0)),
        out_shape=jax.ShapeDtypeStruct((n, d), out_dtype),
        compiler_params=_params("parallel"),
        name="rmsnorm",
    )(x, g.reshape(1, d))


def _matmul_kernel(x_ref, w_ref, o_ref):
    o_ref[...] = _dot(x_ref[...], w_ref[...])


def _matmul(x, w, tm, tn):
    n, k = x.shape
    m = w.shape[1]
    return pl.pallas_call(
        _matmul_kernel,
        grid=(n // tm, m // tn),
        in_specs=[pl.BlockSpec((tm, k), lambda i, j: (i, 0)), pl.BlockSpec((k, tn), lambda i, j: (0, j))],
        out_specs=pl.BlockSpec((tm, tn), lambda i, j: (i, j)),
        out_shape=jax.ShapeDtypeStruct((n, m), f32),
        compiler_params=_params("parallel", "arbitrary"),
        name="in_proj",
    )(x, w)


def _sgu_kernel(z_ref, g_ref, b_ref, ws_ref, bs_ref, ya_ref, vn_ref):
    z = _gelu(z_ref[...])
    u = z[:, :W_A]
    v = z[:, W_A:]
    mu = jnp.mean(v, axis=-1, keepdims=True)
    vc = v - mu
    var = jnp.mean(vc * vc, axis=-1, keepdims=True)
    vn = vc * lax.rsqrt(var + LN_EPS) * g_ref[...] + b_ref[...]
    vn_ref[...] = vn
    for h in range(H_A):
        sl = slice(h * DA, (h + 1) * DA)
        mixed = _dot(ws_ref[h], vn[:, sl].astype(bf16)) + bs_ref[h]
        ya_ref[:, sl] = (u[:, sl] * mixed).astype(bf16)


def _sgu(proj, ln_g, ln_b, ws_m, bs_b):
    n = proj.shape[0]
    col = W_RWKV_PAD // (2 * W_A)
    return pl.pallas_call(
        _sgu_kernel,
        grid=(n // CHUNK_A,),
        in_specs=[
            pl.BlockSpec((CHUNK_A, 2 * W_A), lambda c: (c, col)),
            pl.BlockSpec((1, W_A), lambda c: (0, 0)),
            pl.BlockSpec((1, W_A), lambda c: (0, 0)),
            pl.BlockSpec((H_A, CHUNK_A, CHUNK_A), lambda c: (0, 0, 0)),
            pl.BlockSpec((H_A, CHUNK_A, DA), lambda c: (0, 0, 0)),
        ],
        out_specs=[pl.BlockSpec((CHUNK_A, W_A), lambda c: (c, 0)), pl.BlockSpec((CHUNK_A, W_A), lambda c: (c, 0))],
        out_shape=[jax.ShapeDtypeStruct((n, W_A), bf16), jax.ShapeDtypeStruct((n, W_A), f32)],
        compiler_params=_params("parallel"),
        name="spatial_gate",
    )(proj, ln_g.reshape(1, W_A), ln_b.reshape(1, W_A), ws_m, bs_b)


def _rwkv_prep_kernel(p_ref, pb_ref, mu_ref, w0_ref, wup_ref, a0_ref, aup_ref, gup_ref, kk_ref, ka_ref,
                      r_ref, ld_ref, kp_ref, v_ref, kkr_ref, br_ref, g_ref, *, tm, seq_len, v_lo, v_hi):
    i = pl.program_id(0)
    row = lax.broadcasted_iota(jnp.int32, (tm, W_B), 0)
    pos = (row + i * tm) & (seq_len - 1)
    valid = jnp.where((pos >= v_lo) & (pos < v_hi), 1.0, 0.0).astype(f32)

    def shifted(c):
        sl = slice(c * W_B, (c + 1) * W_B)
        pc = p_ref[:, sl]
        pr = pltpu.roll(pc, 1, 0)
        pr = jnp.where(row == 0, pb_ref[7:8, sl], pr)
        pr = jnp.where(pos == 0, 0.0, pr)
        return pc + (pr - pc) * mu_ref[:, sl]

    ql = shifted(3)
    wa_in = ql[:, :128]
    w = -jax.nn.softplus(-(w0_ref[...] + _dot(jnp.tanh(wa_in), wup_ref[...], HI))) - 0.5
    ld_ref[...] = -jnp.exp(w) * valid
    a_gate = jax.nn.sigmoid(a0_ref[...] + _dot(wa_in, aup_ref[...], HI))
    g_ref[...] = _dot(jax.nn.sigmoid(ql[:, 128:384]), gup_ref[...], HI)
    r_ref[...] = shifted(0) * valid
    k = shifted(1) * valid
    v_ref[...] = shifted(2) * valid
    kkr = k * kk_ref[...]
    kkr_ref[...] = kkr
    br_ref[...] = kkr * a_gate
    kp_ref[...] = k * (1.0 + (a_gate - 1.0) * ka_ref[...])


def _rwkv_prep(proj, wts, tm, seq_len, v_lo, v_hi):
    n = proj.shape[0]
    row = lambda i: (i, 0)
    fix = lambda i: (0, 0)
    vec = pl.BlockSpec((1, W_B), fix)
    out = pl.BlockSpec((tm, W_B), row)
    return pl.pallas_call(
        functools.partial(_rwkv_prep_kernel, tm=tm, seq_len=seq_len, v_lo=v_lo, v_hi=v_hi),
        grid=(n // tm,),
        in_specs=[
            pl.BlockSpec((tm, W_RWKV_PAD), row),
            pl.BlockSpec((8, W_RWKV_PAD), lambda i: (jnp.maximum(i * (tm // 8) - 1, 0), 0)),
            pl.BlockSpec((1, W_RWKV_PAD), fix),
            vec, pl.BlockSpec((128, W_B), fix), vec, pl.BlockSpec((128, W_B), fix), pl.BlockSpec((256, W_B), fix),
            vec, vec,
        ],
        out_specs=[out] * 7,
        out_shape=[jax.ShapeDtypeStruct((n, W_B), f32)] * 7,
        compiler_params=_params("parallel"),
        name="rwkv_prep",
    )(proj, proj, wts["mu"], wts["w0"], wts["w_up"], wts["a0"], wts["a_up"], wts["g_up"], wts["k_k"], wts["k_a"])


def _rwkv_chunk_kernel(r_ref, ld_ref, kp_ref, v_ref, kkr_ref, br_ref, rh_ref, y0_ref, mg_ref, *, C):
    ri = lax.broadcasted_iota(jnp.int32, (C, C), 0)
    ci = lax.broadcasted_iota(jnp.int32, (C, C), 1)
    incl = ri >= ci
    strict = ri > ci
    tri = jnp.where(incl, 1.0, 0.0).astype(f32)
    eye_c = jnp.where(ri == ci, 1.0, 0.0).astype(f32)
    er = lax.broadcasted_iota(jnp.int32, (HEAD_B, HEAD_B), 0)
    ec = lax.broadcasted_iota(jnp.int32, (HEAD_B, HEAD_B), 1)
    rh_parts, y0_parts = [], []
    for hh in range(2):
        sl = slice(hh * HEAD_B, (hh + 1) * HEAD_B)
        r, ld, kp, v = r_ref[:, sl], ld_ref[:, sl], kp_ref[:, sl], v_ref[:, sl]
        kkr, br = kkr_ref[:, sl], br_ref[:, sl]
        nrm = lax.rsqrt(jnp.maximum(jnp.sum(kkr * kkr, axis=-1, keepdims=True), 1e-24))
        a = -(kkr * nrm)
        b = br * nrm
        L = _dot(tri, ld, HI)
        Lc = L[C - 1:C, :]
        e_neg = jnp.exp(-L)
        e_tail = jnp.exp(Lc - L)
        at = a * jnp.exp(L - ld)
        kt = kp * e_neg
        bt = b * e_neg
        rt = r * jnp.exp(L)
        kb = kp * e_tail
        bb = b * e_tail
        a_ab = jnp.where(strict, _dot_nt(at, bt, HI), 0.0)
        a_ak = jnp.where(strict, _dot_nt(at, kt, HI), 0.0)
        a_rk = jnp.where(incl, _dot_nt(rt, kt, HI), 0.0)
        a_rb = jnp.where(incl, _dot_nt(rt, bt, HI), 0.0)
        t_inv = eye_c + a_ab
        pw = a_ab
        span = 2
        while span < C:
            pw = _dot(pw, pw, HI)
            t_inv = t_inv + _dot(t_inv, pw, HI)
            span *= 2
        ah = _dot(t_inv, at, HI)
        u0 = _dot(t_inv, _dot(a_ak, v, HI), HI)
        rh_parts.append(rt + _dot(a_rb, ah, HI))
        y0_parts.append(_dot(a_rk, v, HI) + _dot(a_rb, u0, HI))
        mt = jnp.where(er == ec, jnp.exp(Lc), 0.0) + _dot_tn(ah, bb, HI)
        gt = _dot_tn(v, kb, HI) + _dot_tn(u0, bb, HI)
        mg_ref[0, hh, :, 0:HEAD_B] = mt
        mg_ref[0, hh, :, HEAD_B:2 * HEAD_B] = gt
    rh_ref[...] = jnp.concatenate(rh_parts, axis=-1)
    y0_ref[...] = jnp.concatenate(y0_parts, axis=-1)


def _rwkv_chunk(r, ld, kp, v, kkr, br, C):
    n = r.shape[0]
    blk = pl.BlockSpec((C, 2 * HEAD_B), lambda c, h: (c, h))
    return pl.pallas_call(
        functools.partial(_rwkv_chunk_kernel, C=C),
        grid=(n // C, H_B // 2),
        in_specs=[blk] * 6,
        out_specs=[blk, blk, pl.BlockSpec((1, 2, HEAD_B, 2 * HEAD_B), lambda c, h: (c, h, 0, 0))],
        out_shape=[jax.ShapeDtypeStruct((n, W_B), f32), jax.ShapeDtypeStruct((n, W_B), f32),
                   jax.ShapeDtypeStruct((n // C, H_B, HEAD_B, 2 * HEAD_B), f32)],
        compiler_params=_params("parallel", "parallel"),
        name="rwkv_chunk",
    )(r, ld, kp, v, kkr, br)


def _rwkv_scan_kernel(s0_ref, rh_ref, y0_ref, mg_ref, y_ref, s_out_ref, s_ref, *, bb):
    c = pl.program_id(1)

    @pl.when(c == 0)
    def _():
        s_ref[...] = s0_ref[...]

    for q in range(bb):
        for h in range(H_B):
            sl = slice(h * HEAD_B, (h + 1) * HEAD_B)
            s = s_ref[q, h]
            y_ref[q, :, sl] = _dot_nt(rh_ref[q, :, sl], s, HI) + y0_ref[q, :, sl]
            s_ref[q, h] = _dot(s, mg_ref[q, 0, h, :, 0:HEAD_B], HI) + mg_ref[q, 0, h, :, HEAD_B:2 * HEAD_B]

    @pl.when(c == pl.num_programs(1) - 1)
    def _():
        s_out_ref[...] = s_ref[...]


def _rwkv_scan(s0, rh, y0, mg, n_seq, C, bb):
    t_seq = rh.shape[0] // n_seq
    n_chunks = t_seq // C
    rh3 = rh.reshape(n_seq, t_seq, W_B)
    y03 = y0.reshape(n_seq, t_seq, W_B)
    mg5 = mg.reshape(n_seq, n_chunks, H_B, HEAD_B, 2 * HEAD_B)
    rows = pl.BlockSpec((bb, C, W_B), lambda b, c: (b, c, 0))
    st = pl.BlockSpec((bb, H_B, HEAD_B, HEAD_B), lambda b, c: (b, 0, 0, 0))
    y, s_out = pl.pallas_call(
        functools.partial(_rwkv_scan_kernel, bb=bb),
        grid=(n_seq // bb, n_chunks),
        in_specs=[st, rows, rows, pl.BlockSpec((bb, 1, H_B, HEAD_B, 2 * HEAD_B), lambda b, c: (b, c, 0, 0, 0))],
        out_specs=[rows, st],
        out_shape=[jax.ShapeDtypeStruct((n_seq, t_seq, W_B), f32),
                   jax.ShapeDtypeStruct((n_seq, H_B, HEAD_B, HEAD_B), f32)],
        scratch_shapes=[pltpu.VMEM((bb, H_B, HEAD_B, HEAD_B), f32)],
        compiler_params=_params("parallel", "arbitrary"),
        name="rwkv_scan",
    )(s0, rh3, y03, mg5)
    return y.reshape(n_seq * t_seq, W_B), s_out


def _rwkv_post_kernel(y_ref, r_ref, kp_ref, v_ref, g_ref, rk_ref, gg_ref, gb_ref, o_ref):
    for h in range(H_B):
        sl = slice(h * HEAD_B, (h + 1) * HEAD_B)
        y = y_ref[:, sl]
        mu = jnp.mean(y, axis=-1, keepdims=True)
        yc = y - mu
        var = jnp.mean(yc * yc, axis=-1, keepdims=True)
        yn = yc * lax.rsqrt(var + GN_EPS) * gg_ref[:, sl] + gb_ref[:, sl]
        bonus = jnp.sum(r_ref[:, sl] * kp_ref[:, sl] * rk_ref[:, sl], axis=-1, keepdims=True) * v_ref[:, sl]
        o_ref[:, sl] = ((yn + bonus) * g_ref[:, sl]).astype(bf16)


def _rwkv_post(y, r, kp, v, g, r_k, gn_g, gn_b, tm):
    n = y.shape[0]
    blk = pl.BlockSpec((tm, W_B), lambda i: (i, 0))
    vec = pl.BlockSpec((1, W_B), lambda i: (0, 0))
    return pl.pallas_call(
        _rwkv_post_kernel,
        grid=(n // tm,),
        in_specs=[blk] * 5 + [vec] * 3,
        out_specs=blk,
        out_shape=jax.ShapeDtypeStruct((n, W_B), bf16),
        compiler_params=_params("parallel"),
        name="rwkv_post",
    )(y, r, kp, v, g, r_k.reshape(1, W_B), gn_g.reshape(1, W_B), gn_b.reshape(1, W_B))


def _out_proj_kernel(ya_ref, yb_ref, wa_ref, wb_ref, x_ref, g_ref, h_ref, xn_ref):
    h = x_ref[...] + _dot(ya_ref[...], wa_ref[...]) + _dot(yb_ref[...], wb_ref[...])
    h_ref[...] = h
    y = h * lax.rsqrt(jnp.mean(h * h, axis=-1, keepdims=True) + RMS_EPS)
    xn_ref[...] = (y * g_ref[...]).astype(bf16)


def _out_proj(ya, yb, w_out, x, g, tm):
    n = x.shape[0]
    row = lambda i: (i, 0)
    return pl.pallas_call(
        _out_proj_kernel,
        grid=(n // tm,),
        in_specs=[
            pl.BlockSpec((tm, W_A), row), pl.BlockSpec((tm, W_B), row),
            pl.BlockSpec((W_A, D_MODEL), lambda i: (0, 0)), pl.BlockSpec((W_B, D_MODEL), lambda i: (1, 0)),
            pl.BlockSpec((tm, D_MODEL), row), pl.BlockSpec((1, D_MODEL), lambda i: (0, 0)),
        ],
        out_specs=[pl.BlockSpec((tm, D_MODEL), row), pl.BlockSpec((tm, D_MODEL), row)],
        out_shape=[jax.ShapeDtypeStruct((n, D_MODEL), f32), jax.ShapeDtypeStruct((n, D_MODEL), bf16)],
        compiler_params=_params("parallel"),
        name="out_proj",
    )(ya, yb, w_out, w_out, x, g.reshape(1, D_MODEL))


_PAIRS = [(k1, k2) for k1 in range(TOPK) for k2 in range(TOPK) if (k1 + 1) * (k2 + 1) <= TOPK]


def _top_runs(s):
    vals, cnts = [], []
    cur = s
    for _ in range(TOPK):
        m = jnp.max(cur, axis=0, keepdims=True)
        eq = cur == m
        cnts.append(jnp.sum(jnp.where(eq, 1.0, 0.0), axis=0, keepdims=True))
        vals.append(m)
        cur = jnp.where(eq, NEG_INF, cur)
    return vals, cnts


def _router_kernel(x_ref, wq_ref, sk_ref, s1_ref, e1_ref, s2_ref, e2_ref, thr_ref):
    q = _dot(x_ref[...], wq_ref[...])
    m1, c1, m2, c2 = [], [], [], []
    for h in range(PEER_HEADS):
        o = h * 2 * HALF_KEY
        s1 = _dot_nt(sk_ref[0], q[:, o:o + HALF_KEY], HI)
        s2 = _dot_nt(sk_ref[1], q[:, o + HALF_KEY:o + 2 * HALF_KEY], HI)
        s1_ref[h] = s1
        s2_ref[h] = s2
        va, ca = _top_runs(s1)
        vb, cb = _top_runs(s2)
        m1.append(va); c1.append(ca); m2.append(vb); c2.append(cb)
    stack = lambda lst, k: jnp.concatenate([lst[h][k] for h in range(PEER_HEADS)], axis=0)
    m1 = [stack(m1, k) for k in range(TOPK)]
    c1 = [stack(c1, k) for k in range(TOPK)]
    m2 = [stack(m2, k) for k in range(TOPK)]
    c2 = [stack(c2, k) for k in range(TOPK)]
    vs = [m1[k1] + m2[k2] for k1, k2 in _PAIRS]
    ws = [c1[k1] * c2[k2] for k1, k2 in _PAIRS]
    thr = jnp.full_like(vs[0], NEG_INF)
    for vp in vs:
        cnt = jnp.zeros_like(vp)
        for vq, wq in zip(vs, ws):
            cnt = cnt + jnp.where(vq >= vp, wq, 0.0)
        thr = jnp.maximum(thr, jnp.where(cnt >= float(TOPK), vp, NEG_INF))
    vmax = vs[0]
    z = jnp.zeros_like(vmax)
    for vp, wp in zip(vs, ws):
        z = z + jnp.where(vp >= thr, wp * jnp.exp(vp - vmax), 0.0)
    thr_ref[...] = thr
    zinv = 1.0 / z
    for h in range(PEER_HEADS):
        e1_ref[h] = jnp.exp(s1_ref[h] - m1[0][h:h + 1, :]) * zinv[h:h + 1, :]
        e2_ref[h] = jnp.exp(s2_ref[h] - m2[0][h:h + 1, :])


def _router(xn, w_q, sub_keys, tm):
    n = xn.shape[0]
    big = pl.BlockSpec((PEER_HEADS, N_KEYS, tm), lambda i: (0, 0, i))
    shp = jax.ShapeDtypeStruct((PEER_HEADS, N_KEYS, n), f32)
    return pl.pallas_call(
        _router_kernel,
        grid=(n // tm,),
        in_specs=[
            pl.BlockSpec((tm, D_MODEL), lambda i: (i, 0)),
            pl.BlockSpec((D_MODEL, PEER_HEADS * 2 * HALF_KEY), lambda i: (0, 0)),
            pl.BlockSpec((2, N_KEYS, HALF_KEY), lambda i: (0, 0, 0)),
        ],
        out_specs=[big, big, big, big, pl.BlockSpec((PEER_HEADS, tm), lambda i: (0, i))],
        out_shape=[shp, shp, shp, shp, jax.ShapeDtypeStruct((PEER_HEADS, n), f32)],
        compiler_params=_params("parallel"),
        name="peer_router",
    )(xn, w_q, sub_keys)


def _peer_kernel(x_ref, u_ref, vt_ref, s1_ref, e1_ref, s2_ref, e2_ref, thr_ref, o_ref, acc_ref, *, te):
    j = pl.program_id(1)

    @pl.when(j == 0)
    def _():
        acc_ref[...] = jnp.zeros_like(acc_ref)

    act = _dot_nt(u_ref[...], x_ref[...])
    g = _gelu(act)
    n_i1 = te // N_KEYS
    parts = []
    for a in range(n_i1):
        i1 = j * n_i1 + a
        w = None
        for h in range(PEER_HEADS):
            s1r = s1_ref[h, pl.ds(i1, 1), :]
            e1r = e1_ref[h, pl.ds(i1, 1), :]
            sel = jnp.where(s1r + s2_ref[h] >= thr_ref[h:h + 1, :], e1r * e2_ref[h], 0.0)
            w = sel if w is None else w + sel
        parts.append((w * g[a * N_KEYS:(a + 1) * N_KEYS, :]).astype(bf16))
    coef = jnp.concatenate(parts, axis=0)
    acc_ref[...] += _dot(vt_ref[...], coef)

    @pl.when(j == pl.num_programs(1) - 1)
    def _():
        o_ref[...] = acc_ref[...].T


def _peer(xn, u_bf, vt_bf, s1, e1, s2, e2, thr, tm, te):
    n = xn.shape[0]
    n_exp = u_bf.shape[0]
    tok = pl.BlockSpec((PEER_HEADS, N_KEYS, tm), lambda i, j: (0, 0, i))
    return pl.pallas_call(
        functools.partial(_peer_kernel, te=te),
        grid=(n // tm, n_exp // te),
        in_specs=[
            pl.BlockSpec((tm, D_MODEL), lambda i, j: (i, 0)),
            pl.BlockSpec((te, D_MODEL), lambda i, j: (j, 0)),
            pl.BlockSpec((D_MODEL, te), lambda i, j: (0, j)),
            tok, tok, tok, tok,
            pl.BlockSpec((PEER_HEADS, tm), lambda i, j: (0, i)),
        ],
        out_specs=pl.BlockSpec((tm, D_MODEL), lambda i, j: (i, 0)),
        out_shape=jax.ShapeDtypeStruct((n, D_MODEL), f32),
        scratch_shapes=[pltpu.VMEM((D_MODEL, tm), f32)],
        compiler_params=_params("parallel", "arbitrary"),
        name="peer_dense",
    )(xn, u_bf, vt_bf, s1, e1, s2, e2, thr)


def _final_kernel(h_ref, p_ref, g_ref, o_ref):
    y = h_ref[...] + p_ref[...]
    o_ref[...] = y * lax.rsqrt(jnp.mean(y * y, axis=-1, keepdims=True) + RMS_EPS) * g_ref[...]


def _final(h, p, g, tm):
    n, d = h.shape
    blk = pl.BlockSpec((tm, d), lambda i: (i, 0))
    return pl.pallas_call(
        _final_kernel,
        grid=(n // tm,),
        in_specs=[blk, blk, pl.BlockSpec((1, d), lambda i: (0, 0))],
        out_specs=blk,
        out_shape=jax.ShapeDtypeStruct((n, d), f32),
        compiler_params=_params("parallel"),
        name="final_norm",
    )(h, p, g.reshape(1, d))


def _pad_rows(w, rows, at=0):
    out = jnp.zeros((rows, w.shape[1]), w.dtype)
    return lax.dynamic_update_slice(out, w, (at, 0))


def _prep_weights(w_in, ws, bs, mu, w0, w_up, a0, a_up, g_up, k_k, k_a):
    pad_c = W_RWKV_PAD - W_RWKV_IN
    w_cat = jnp.concatenate([w_in[:, 2 * W_A:], jnp.zeros((D_MODEL, pad_c), f32), w_in[:, :2 * W_A]], axis=1).astype(bf16)
    mu_p = jnp.concatenate([mu, jnp.zeros((pad_c,), f32)]).reshape(1, W_RWKV_PAD)
    tril = jnp.tril(jnp.ones((CHUNK_A, CHUNK_A), bool))
    ws_prompt = jnp.where(tril[None], ws, 0.0).astype(bf16)
    bs_prompt = jnp.broadcast_to(bs[:, :, None], (H_A, CHUNK_A, DA))
    n_tok = 4
    e = jnp.zeros((H_A, SAMPLE_ROWS, SAMPLE_ROWS), f32).at[:, 1:1 + n_tok, 1:1 + n_tok].set(
        jnp.where(tril[:n_tok, :n_tok][None], ws[:, :n_tok, :n_tok], 0.0))
    reps = CHUNK_A // SAMPLE_ROWS
    ws_sample = jnp.einsum("pq,hab->hpaqb", jnp.eye(reps, dtype=f32), e).reshape(H_A, CHUNK_A, CHUNK_A).astype(bf16)
    b8 = jnp.zeros((H_A, SAMPLE_ROWS), f32).at[:, 1:1 + n_tok].set(bs[:, :n_tok])
    bs_sample = jnp.broadcast_to(jnp.tile(b8, (1, reps))[:, :, None], (H_A, CHUNK_A, DA))
    rw = {
        "mu": mu_p, "w0": w0.reshape(1, W_B), "a0": a0.reshape(1, W_B),
        "w_up": _pad_rows(w_up, 128, 0), "a_up": _pad_rows(a_up, 128, LORA_W), "g_up": _pad_rows(g_up, 256, 0),
        "k_k": k_k.reshape(1, W_B), "k_a": k_a.reshape(1, W_B),
    }
    return w_cat, (ws_prompt, bs_prompt), (ws_sample, bs_sample), rw


def _mix(xn_rows, s0, n_seq, seq_len, v_lo, v_hi, C, bb, tm_mm, tm_tok, w_cat, sg, rw, ln_v_g, ln_v_b, r_k, gn_g, gn_b):
    proj = _matmul(xn_rows, w_cat, tm_mm, 512)
    ya, vn = _sgu(proj, ln_v_g, ln_v_b, *sg)
    r, ld, kp, v, kkr, br, g = _rwkv_prep(proj, rw, tm_tok, seq_len, v_lo, v_hi)
    rh, y0, mg = _rwkv_chunk(r, ld, kp, v, kkr, br, C)
    y, s_new = _rwkv_scan(s0, rh, y0, mg, n_seq, C, bb)
    yb = _rwkv_post(y, r, kp, v, g, r_k, gn_g, gn_b, tm_tok)
    return ya, yb, vn, s_new


def _channel_mix(x, ya, yb, w_out_bf, norm2_g, w_q_bf, sub_keys, u_bf, vt_bf, final_g, tm_peer, te):
    h, xn2 = _out_proj(ya, yb, w_out_bf, x, norm2_g, 256)
    s1, e1, s2, e2, thr = _router(xn2, w_q_bf, sub_keys, 256)
    p = _peer(xn2, u_bf, vt_bf, s1, e1, s2, e2, thr, tm_peer, te)
    return _final(h, p, final_g, 512)


def kernel(x_prompt, x_sample, state_wkv, state_shift, norm1_g, w_in, ln_v_g, ln_v_b, ws, bs, mu, w0, w_up, a0, a_up, g_up, k_k, k_a, r_k, gn_g, gn_b, w_out, norm2_g, w_q, sub_keys, u_tab, v_tab, final_g):
    assert norm1_g.shape[0] == 1, "single trunk layer"
    bp, t_p, d = x_prompt.shape
    bs_n, t_s, _ = x_sample.shape
    w_cat, sg_p, sg_s, rw = _prep_weights(w_in[0], ws[0], bs[0], mu[0], w0[0], w_up[0], a0[0], a_up[0], g_up[0], k_k[0], k_a[0])
    w_out_bf = w_out[0].astype(bf16)
    w_q_bf = w_q[0].astype(bf16)
    u_bf = u_tab[0].astype(bf16)
    vt_bf = v_tab[0].astype(bf16).T
    shared = (ln_v_g[0], ln_v_b[0], r_k[0].reshape(W_B), gn_g[0], gn_b[0])
    cm = (w_out_bf, norm2_g[0], w_q_bf, sub_keys[0], u_bf, vt_bf, final_g)

    xp = x_prompt.reshape(bp * t_p, d)
    xn_p = _rmsnorm(xp, norm1_g[0], bf16, 512)
    s0_p = jnp.zeros((bp, H_B, HEAD_B, HEAD_B), f32)
    ya, yb, vn_p, wkv_p = _mix(xn_p, s0_p, bp, t_p, 0, t_p, 64, bp, 1024, 256, w_cat, sg_p, rw, *shared)
    y_p = _channel_mix(xp, ya, yb, *cm, 512, 512)

    xs = x_sample.reshape(bs_n * t_s, d)
    xn_s = _rmsnorm(xs, norm1_g[0], bf16, 512).reshape(bs_n, t_s, d)
    rows = jnp.concatenate([state_shift[0].astype(bf16)[:, None], xn_s,
                            jnp.zeros((bs_n, SAMPLE_ROWS - 1 - t_s, d), bf16)], axis=1).reshape(bs_n * SAMPLE_ROWS, d)
    ya8, yb8, vn8, wkv_s = _mix(rows, state_wkv[0], bs_n, SAMPLE_ROWS, 1, 1 + t_s, SAMPLE_ROWS, 8, 512, 256,
                                w_cat, sg_s, rw, *shared)
    tok = lambda a: a.reshape(bs_n, SAMPLE_ROWS, -1)[:, 1:1 + t_s].reshape(bs_n * t_s, -1)
    y_s = _channel_mix(xs, tok(ya8), tok(yb8), *cm, 512, 512)

    last = jnp.concatenate([x_prompt[:, -1], x_sample[:, -1]], axis=0)
    n_last = last.shape[0]
    pad = (-n_last) % 8
    last = jnp.concatenate([last, jnp.zeros((pad, d), f32)], axis=0)
    xn_last = _rmsnorm(last, norm1_g[0], f32, n_last + pad)

    chunk_rows = t_p - ((t_p - 1) // CHUNK_A) * CHUNK_A
    return (
        y_p.reshape(bp, t_p, d),
        y_s.reshape(bs_n, t_s, d),
        wkv_p[None],
        xn_last[:bp][None],
        vn_p.reshape(bp, t_p, W_A)[:, t_p - chunk_rows:][None],
        wkv_s[None],
        xn_last[bp:bp + bs_n][None],
        vn8.reshape(bs_n, SAMPLE_ROWS, W_A)[:, 1:1 + t_s][None],
    )
```

```python
import functools
import math

import jax
import jax.numpy as jnp
from jax import lax
from jax.experimental import pallas as pl
from jax.experimental.pallas import tpu as pltpu

f32 = jnp.float32
bf16 = jnp.bfloat16

D_MODEL = 2048
W_A = 1024
W_B = 1024
H_A = 8
DA = 128
CHUNK_A = 128
HEAD_B = 64
H_B = 16
LORA_W = 64
LORA_A = 64
LORA_G = 160
W_RWKV_IN = 3 * W_B + LORA_W + LORA_A + LORA_G
W_RWKV_PAD = 4 * W_B
W_CAT = W_RWKV_PAD + 2 * W_A
N_KEYS = 128
PEER_HEADS = 8
HALF_KEY = 128
TOPK = 16
RMS_EPS = 1e-6
LN_EPS = 1e-5
GN_EPS = 6.4e-4
SAMPLE_ROWS = 8
VMEM_LIMIT = 56 * 1024 * 1024
HI = lax.Precision.HIGHEST
INV_SQRT2 = 1.0 / math.sqrt(2.0)
NEG_INF = float("-inf")


def _params(*sem):
    return pltpu.CompilerParams(dimension_semantics=sem, vmem_limit_bytes=VMEM_LIMIT)


def _gelu(x):
    return 0.5 * x * (1.0 + lax.erf(x * INV_SQRT2))


def _dot(a, b, prec=None):
    return jnp.dot(a, b, preferred_element_type=f32, precision=prec)


def _dot_nt(a, b, prec=None):
    return lax.dot_general(a, b, (((1,), (1,)), ((), ())), preferred_element_type=f32, precision=prec)


def _dot_tn(a, b, prec=None):
    return lax.dot_general(a, b, (((0,), (0,)), ((), ())), preferred_element_type=f32, precision=prec)


def _rmsnorm_kernel(x_ref, g_ref, o_ref):
    x = x_ref[...]
    y = x * lax.rsqrt(jnp.mean(x * x, axis=-1, keepdims=True) + RMS_EPS)
    o_ref[...] = (y * g_ref[...]).astype(o_ref.dtype)


def _rmsnorm(x, g, out_dtype, tm):
    n, d = x.shape
    return pl.pallas_call(
        _rmsnorm_kernel,
        grid=(n // tm,),
        in_specs=[pl.BlockSpec((tm, d), lambda i: (i, 0)), pl.BlockSpec((1, d), lambda i: (0, 0))],
        out_specs=pl.BlockSpec((tm, d), lambda i: (i, 0)),
        out_shape=jax.ShapeDtypeStruct((n, d), out_dtype),
        compiler_params=_params("parallel"),
        name="rmsnorm",
    )(x, g.reshape(1, d))


def _matmul_kernel(x_ref, w_ref, o_ref):
    o_ref[...] = _dot(x_ref[...], w_ref[...])


def _matmul(x, w, tm, tn):
    n, k = x.shape
    m = w.shape[1]
    return pl.pallas_call(
        _matmul_kernel,
        grid=(n // tm, m // tn),
        in_specs=[pl.BlockSpec((tm, k), lambda i, j: (i, 0)), pl.BlockSpec((k, tn), lambda i, j: (0, j))],
        out_specs=pl.BlockSpec((tm, tn), lambda i, j: (i, j)),
        out_shape=jax.ShapeDtypeStruct((n, m), f32),
        compiler_params=_params("parallel", "arbitrary"),
        name="in_proj",
    )(x, w)


def _sgu_kernel(z_ref, g_ref, b_ref, ws_ref, bs_ref, ya_ref, vn_ref):
    z = _gelu(z_ref[...])
    u = z[:, :W_A]
    v = z[:, W_A:]
    mu = jnp.mean(v, axis=-1, keepdims=True)
    vc = v - mu
    var = jnp.mean(vc * vc, axis=-1, keepdims=True)
    vn = vc * lax.rsqrt(var + LN_EPS) * g_ref[...] + b_ref[...]
    vn_ref[...] = vn
    for h in range(H_A):
        sl = slice(h * DA, (h + 1) * DA)
        mixed = _dot(ws_ref[h], vn[:, sl].astype(bf16)) + bs_ref[h]
        ya_ref[:, sl] = (u[:, sl] * mixed).astype(bf16)


def _sgu(proj, ln_g, ln_b, ws_m, bs_b):
    n = proj.shape[0]
    col = W_RWKV_PAD // (2 * W_A)
    return pl.pallas_call(
        _sgu_kernel,
        grid=(n // CHUNK_A,),
        in_specs=[
            pl.BlockSpec((CHUNK_A, 2 * W_A), lambda c: (c, col)),
            pl.BlockSpec((1, W_A), lambda c: (0, 0)),
            pl.BlockSpec((1, W_A), lambda c: (0, 0)),
            pl.BlockSpec((H_A, CHUNK_A, CHUNK_A), lambda c: (0, 0, 0)),
            pl.BlockSpec((H_A, CHUNK_A, DA), lambda c: (0, 0, 0)),
        ],
        out_specs=[pl.BlockSpec((CHUNK_A, W_A), lambda c: (c, 0)), pl.BlockSpec((CHUNK_A, W_A), lambda c: (c, 0))],
        out_shape=[jax.ShapeDtypeStruct((n, W_A), bf16), jax.ShapeDtypeStruct((n, W_A), f32)],
        compiler_params=_params("parallel"),
        name="spatial_gate",
    )(proj, ln_g.reshape(1, W_A), ln_b.reshape(1, W_A), ws_m, bs_b)


def _rwkv_prep_kernel(p_ref, pb_ref, mu_ref, w0_ref, wup_ref, a0_ref, aup_ref, gup_ref, kk_ref, ka_ref,
                      r_ref, ld_ref, kp_ref, v_ref, kkr_ref, br_ref, g_ref, *, tm, seq_len, v_lo, v_hi):
    i = pl.program_id(0)
    row = lax.broadcasted_iota(jnp.int32, (tm, W_B), 0)
    pos = (row + i * tm) & (seq_len - 1)
    valid = jnp.where((pos >= v_lo) & (pos < v_hi), 1.0, 0.0).astype(f32)

    def shifted(c):
        sl = slice(c * W_B, (c + 1) * W_B)
        pc = p_ref[:, sl]
        pr = pltpu.roll(pc, 1, 0)
        pr = jnp.where(row == 0, pb_ref[7:8, sl], pr)
        pr = jnp.where(pos == 0, 0.0, pr)
        return pc + (pr - pc) * mu_ref[:, sl]

    ql = shifted(3)
    wa_in = ql[:, :128]
    w = -jax.nn.softplus(-(w0_ref[...] + _dot(jnp.tanh(wa_in), wup_ref[...], HI))) - 0.5
    ld_ref[...] = -jnp.exp(w) * valid
    a_gate = jax.nn.sigmoid(a0_ref[...] + _dot(wa_in, aup_ref[...], HI))
    g_ref[...] = _dot(jax.nn.sigmoid(ql[:, 128:384]), gup_ref[...], HI)
    r_ref[...] = shifted(0) * valid
    k = shifted(1) * valid
    v_ref[...] = shifted(2) * valid
    kkr = k * kk_ref[...]
    kkr_ref[...] = kkr
    br_ref[...] = kkr * a_gate
    kp_ref[...] = k * (1.0 + (a_gate - 1.0) * ka_ref[...])


def _rwkv_prep(proj, wts, tm, seq_len, v_lo, v_hi):
    n = proj.shape[0]
    row = lambda i: (i, 0)
    fix = lambda i: (0, 0)
    vec = pl.BlockSpec((1, W_B), fix)
    out = pl.BlockSpec((tm, W_B), row)
    return pl.pallas_call(
        functools.partial(_rwkv_prep_kernel, tm=tm, seq_len=seq_len, v_lo=v_lo, v_hi=v_hi),
        grid=(n // tm,),
        in_specs=[
            pl.BlockSpec((tm, W_RWKV_PAD), row),
            pl.BlockSpec((8, W_RWKV_PAD), lambda i: (jnp.maximum(i * (tm // 8) - 1, 0), 0)),
            pl.BlockSpec((1, W_RWKV_PAD), fix),
            vec, pl.BlockSpec((128, W_B), fix), vec, pl.BlockSpec((128, W_B), fix), pl.BlockSpec((256, W_B), fix),
            vec, vec,
        ],
        out_specs=[out] * 7,
        out_shape=[jax.ShapeDtypeStruct((n, W_B), f32)] * 7,
        compiler_params=_params("parallel"),
        name="rwkv_prep",
    )(proj, proj, wts["mu"], wts["w0"], wts["w_up"], wts["a0"], wts["a_up"], wts["g_up"], wts["k_k"], wts["k_a"])


def _rwkv_chunk_kernel(r_ref, ld_ref, kp_ref, v_ref, kkr_ref, br_ref, rh_ref, y0_ref, mg_ref, *, C):
    ri = lax.broadcasted_iota(jnp.int32, (C, C), 0)
    ci = lax.broadcasted_iota(jnp.int32, (C, C), 1)
    incl = ri >= ci
    strict = ri > ci
    tri = jnp.where(incl, 1.0, 0.0).astype(f32)
    eye_c = jnp.where(ri == ci, 1.0, 0.0).astype(f32)
    eye_h = (lax.broadcasted_iota(jnp.int32, (HEAD_B, HEAD_B), 0)
             == lax.broadcasted_iota(jnp.int32, (HEAD_B, HEAD_B), 1))
    ld = ld_ref[...]
    L = _dot(tri, ld, HI)
    Lc = L[C - 1:C, :]
    e_pos = jnp.exp(L)
    e_neg = jnp.exp(-L)
    e_tail = jnp.exp(Lc - L)
    e_prev = jnp.exp(L - ld)
    wc = jnp.exp(Lc)
    sls = [slice(h * HEAD_B, (h + 1) * HEAD_B) for h in range(H_B)]
    at, rt, vb, kb, bb, lhs, rhs = [], [], [], [], [], [], []
    for sl in sls:
        kkr = kkr_ref[:, sl]
        nrm = lax.rsqrt(jnp.maximum(jnp.sum(kkr * kkr, axis=-1, keepdims=True), 1e-24))
        a = -(kkr * nrm)
        b = br_ref[:, sl] * nrm
        kp = kp_ref[:, sl]
        vb.append(v_ref[:, sl].astype(bf16))
        at.append(a * e_prev[:, sl])
        rt.append(r_ref[:, sl] * e_pos[:, sl])
        kb.append((kp * e_tail[:, sl]).astype(bf16))
        bb.append((b * e_tail[:, sl]).astype(bf16))
        lhs.append(jnp.concatenate([at[-1], rt[-1]], axis=0).astype(bf16))
        rhs.append(jnp.concatenate([b * e_neg[:, sl], kp * e_neg[:, sl]], axis=0).astype(bf16))
    big = [_dot_nt(l, r) for l, r in zip(lhs, rhs)]
    a_ab = [jnp.where(strict, x[:C, :C], 0.0) for x in big]
    a_ak = [jnp.where(strict, x[:C, C:], 0.0).astype(bf16) for x in big]
    a_rb = [jnp.where(incl, x[C:, :C], 0.0).astype(bf16) for x in big]
    a_rk = [jnp.where(incl, x[C:, C:], 0.0).astype(bf16) for x in big]
    akv = [_dot(x, v) for x, v in zip(a_ak, vb)]
    y0a = [_dot(x, v) for x, v in zip(a_rk, vb)]
    gta = [_dot_tn(v, k) for v, k in zip(vb, kb)]
    t_inv = [eye_c + x for x in a_ab]
    pw = [x.astype(bf16) for x in a_ab]
    span = 2
    while span < C:
        pw = [_dot(x, x).astype(bf16) for x in pw]
        t_inv = [t + _dot(t.astype(bf16), x) for t, x in zip(t_inv, pw)]
        span *= 2
    au = [_dot(t.astype(bf16), jnp.concatenate([x, y], axis=1).astype(bf16)).astype(bf16)
          for t, x, y in zip(t_inv, at, akv)]
    ru = [_dot(x, y) for x, y in zip(a_rb, au)]
    lowrank = [_dot_tn(x, y) for x, y in zip(au, bb)]
    for h, sl in enumerate(sls):
        rh_ref[:, sl] = rt[h] + ru[h][:, :HEAD_B]
        y0_ref[:, sl] = y0a[h] + ru[h][:, HEAD_B:]
        mg_ref[0, h, :, 0:HEAD_B] = jnp.where(eye_h, wc[:, sl], 0.0) + lowrank[h][:HEAD_B]
        mg_ref[0, h, :, HEAD_B:2 * HEAD_B] = gta[h] + lowrank[h][HEAD_B:]


def _rwkv_chunk(r, ld, kp, v, kkr, br, C):
    n = r.shape[0]
    blk = pl.BlockSpec((C, W_B), lambda c: (c, 0))
    return pl.pallas_call(
        functools.partial(_rwkv_chunk_kernel, C=C),
        grid=(n // C,),
        in_specs=[blk] * 6,
        out_specs=[blk, blk, pl.BlockSpec((1, H_B, HEAD_B, 2 * HEAD_B), lambda c: (c, 0, 0, 0))],
        out_shape=[jax.ShapeDtypeStruct((n, W_B), f32), jax.ShapeDtypeStruct((n, W_B), f32),
                   jax.ShapeDtypeStruct((n // C, H_B, HEAD_B, 2 * HEAD_B), f32)],
        compiler_params=_params("parallel"),
        name="rwkv_chunk",
    )(r, ld, kp, v, kkr, br)


def _rwkv_scan_kernel(s0_ref, rh_ref, y0_ref, mg_ref, y_ref, s_out_ref, s_ref, *, bb):
    c = pl.program_id(1)

    @pl.when(c == 0)
    def _():
        s_ref[...] = s0_ref[...]

    qh = [(q, h, slice(h * HEAD_B, (h + 1) * HEAD_B)) for q in range(bb) for h in range(H_B)]
    sb = [s_ref[q, h].astype(bf16) for q, h, _ in qh]
    ys = [_dot_nt(rh_ref[q, :, sl].astype(bf16), s) for (q, h, sl), s in zip(qh, sb)]
    sn = [_dot(s, mg_ref[q, 0, h, :, 0:HEAD_B].astype(bf16)) for (q, h, sl), s in zip(qh, sb)]
    for (q, h, sl), y, s in zip(qh, ys, sn):
        y_ref[q, :, sl] = y + y0_ref[q, :, sl]
        s_ref[q, h] = s + mg_ref[q, 0, h, :, HEAD_B:2 * HEAD_B]

    @pl.when(c == pl.num_programs(1) - 1)
    def _():
        s_out_ref[...] = s_ref[...]


def _rwkv_scan(s0, rh, y0, mg, n_seq, C, bb):
    t_seq = rh.shape[0] // n_seq
    n_chunks = t_seq // C
    rh3 = rh.reshape(n_seq, t_seq, W_B)
    y03 = y0.reshape(n_seq, t_seq, W_B)
    mg5 = mg.reshape(n_seq, n_chunks, H_B, HEAD_B, 2 * HEAD_B)
    rows = pl.BlockSpec((bb, C, W_B), lambda b, c: (b, c, 0))
    st = pl.BlockSpec((bb, H_B, HEAD_B, HEAD_B), lambda b, c: (b, 0, 0, 0))
    y, s_out = pl.pallas_call(
        functools.partial(_rwkv_scan_kernel, bb=bb),
        grid=(n_seq // bb, n_chunks),
        in_specs=[st, rows, rows, pl.BlockSpec((bb, 1, H_B, HEAD_B, 2 * HEAD_B), lambda b, c: (b, c, 0, 0, 0))],
        out_specs=[rows, st],
        out_shape=[jax.ShapeDtypeStruct((n_seq, t_seq, W_B), f32),
                   jax.ShapeDtypeStruct((n_seq, H_B, HEAD_B, HEAD_B), f32)],
        scratch_shapes=[pltpu.VMEM((bb, H_B, HEAD_B, HEAD_B), f32)],
        compiler_params=_params("parallel", "arbitrary"),
        name="rwkv_scan",
    )(s0, rh3, y03, mg5)
    return y.reshape(n_seq * t_seq, W_B), s_out


def _rwkv_post_kernel(y_ref, r_ref, kp_ref, v_ref, g_ref, rk_ref, gg_ref, gb_ref, o_ref):
    for h in range(H_B):
        sl = slice(h * HEAD_B, (h + 1) * HEAD_B)
        y = y_ref[:, sl]
        mu = jnp.mean(y, axis=-1, keepdims=True)
        yc = y - mu
        var = jnp.mean(yc * yc, axis=-1, keepdims=True)
        yn = yc * lax.rsqrt(var + GN_EPS) * gg_ref[:, sl] + gb_ref[:, sl]
        bonus = jnp.sum(r_ref[:, sl] * kp_ref[:, sl] * rk_ref[:, sl], axis=-1, keepdims=True) * v_ref[:, sl]
        o_ref[:, sl] = ((yn + bonus) * g_ref[:, sl]).astype(bf16)


def _rwkv_post(y, r, kp, v, g, r_k, gn_g, gn_b, tm):
    n = y.shape[0]
    blk = pl.BlockSpec((tm, W_B), lambda i: (i, 0))
    vec = pl.BlockSpec((1, W_B), lambda i: (0, 0))
    return pl.pallas_call(
        _rwkv_post_kernel,
        grid=(n // tm,),
        in_specs=[blk] * 5 + [vec] * 3,
        out_specs=blk,
        out_shape=jax.ShapeDtypeStruct((n, W_B), bf16),
        compiler_params=_params("parallel"),
        name="rwkv_post",
    )(y, r, kp, v, g, r_k.reshape(1, W_B), gn_g.reshape(1, W_B), gn_b.reshape(1, W_B))


def _out_proj_kernel(ya_ref, yb_ref, wa_ref, wb_ref, x_ref, g_ref, h_ref, xn_ref):
    h = x_ref[...] + _dot(ya_ref[...], wa_ref[...]) + _dot(yb_ref[...], wb_ref[...])
    h_ref[...] = h
    y = h * lax.rsqrt(jnp.mean(h * h, axis=-1, keepdims=True) + RMS_EPS)
    xn_ref[...] = (y * g_ref[...]).astype(bf16)


def _out_proj(ya, yb, w_out, x, g, tm):
    n = x.shape[0]
    row = lambda i: (i, 0)
    return pl.pallas_call(
        _out_proj_kernel,
        grid=(n // tm,),
        in_specs=[
            pl.BlockSpec((tm, W_A), row), pl.BlockSpec((tm, W_B), row),
            pl.BlockSpec((W_A, D_MODEL), lambda i: (0, 0)), pl.BlockSpec((W_B, D_MODEL), lambda i: (1, 0)),
            pl.BlockSpec((tm, D_MODEL), row), pl.BlockSpec((1, D_MODEL), lambda i: (0, 0)),
        ],
        out_specs=[pl.BlockSpec((tm, D_MODEL), row), pl.BlockSpec((tm, D_MODEL), row)],
        out_shape=[jax.ShapeDtypeStruct((n, D_MODEL), f32), jax.ShapeDtypeStruct((n, D_MODEL), bf16)],
        compiler_params=_params("parallel"),
        name="out_proj",
    )(ya, yb, w_out, w_out, x, g.reshape(1, D_MODEL))


_PAIRS = [(k1, k2) for k1 in range(TOPK) for k2 in range(TOPK) if (k1 + 1) * (k2 + 1) <= TOPK]


def _top_runs(s):
    vals, cnts = [], []
    cur = s
    for _ in range(TOPK):
        m = jnp.max(cur, axis=0, keepdims=True)
        eq = cur == m
        cnts.append(jnp.sum(jnp.where(eq, 1.0, 0.0), axis=0, keepdims=True))
        vals.append(m)
        cur = jnp.where(eq, NEG_INF, cur)
    return vals, cnts


def _router_kernel(x_ref, wq_ref, sk_ref, s1_ref, e1_ref, s2_ref, e2_ref, thr_ref):
    q = _dot(x_ref[...], wq_ref[...])
    m1, c1, m2, c2 = [], [], [], []
    for h in range(PEER_HEADS):
        o = h * 2 * HALF_KEY
        s1 = _dot_nt(sk_ref[0], q[:, o:o + HALF_KEY], HI)
        s2 = _dot_nt(sk_ref[1], q[:, o + HALF_KEY:o + 2 * HALF_KEY], HI)
        s1_ref[h] = s1
        s2_ref[h] = s2
        va, ca = _top_runs(s1)
        vb, cb = _top_runs(s2)
        m1.append(va); c1.append(ca); m2.append(vb); c2.append(cb)
    stack = lambda lst, k: jnp.concatenate([lst[h][k] for h in range(PEER_HEADS)], axis=0)
    m1 = [stack(m1, k) for k in range(TOPK)]
    c1 = [stack(c1, k) for k in range(TOPK)]
    m2 = [stack(m2, k) for k in range(TOPK)]
    c2 = [stack(c2, k) for k in range(TOPK)]
    vs = [m1[k1] + m2[k2] for k1, k2 in _PAIRS]
    ws = [c1[k1] * c2[k2] for k1, k2 in _PAIRS]
    thr = jnp.full_like(vs[0], NEG_INF)
    for vp in vs:
        cnt = jnp.zeros_like(vp)
        for vq, wq in zip(vs, ws):
            cnt = cnt + jnp.where(vq >= vp, wq, 0.0)
        thr = jnp.maximum(thr, jnp.where(cnt >= float(TOPK), vp, NEG_INF))
    vmax = vs[0]
    z = jnp.zeros_like(vmax)
    for vp, wp in zip(vs, ws):
        z = z + jnp.where(vp >= thr, wp * jnp.exp(vp - vmax), 0.0)
    thr_ref[...] = thr
    zinv = 1.0 / z
    for h in range(PEER_HEADS):
        e1_ref[h] = jnp.exp(s1_ref[h] - m1[0][h:h + 1, :]) * zinv[h:h + 1, :]
        e2_ref[h] = jnp.exp(s2_ref[h] - m2[0][h:h + 1, :])


def _router(xn, w_q, sub_keys, tm):
    n = xn.shape[0]
    big = pl.BlockSpec((PEER_HEADS, N_KEYS, tm), lambda i: (0, 0, i))
    shp = jax.ShapeDtypeStruct((PEER_HEADS, N_KEYS, n), f32)
    return pl.pallas_call(
        _router_kernel,
        grid=(n // tm,),
        in_specs=[
            pl.BlockSpec((tm, D_MODEL), lambda i: (i, 0)),
            pl.BlockSpec((D_MODEL, PEER_HEADS * 2 * HALF_KEY), lambda i: (0, 0)),
            pl.BlockSpec((2, N_KEYS, HALF_KEY), lambda i: (0, 0, 0)),
        ],
        out_specs=[big, big, big, big, pl.BlockSpec((PEER_HEADS, tm), lambda i: (0, i))],
        out_shape=[shp, shp, shp, shp, jax.ShapeDtypeStruct((PEER_HEADS, n), f32)],
        compiler_params=_params("parallel"),
        name="peer_router",
    )(xn, w_q, sub_keys)


def _peer_kernel(x_ref, u_ref, vt_ref, s1_ref, e1_ref, s2_ref, e2_ref, thr_ref, o_ref, acc_ref, *, te):
    j = pl.program_id(1)

    @pl.when(j == 0)
    def _():
        acc_ref[...] = jnp.zeros_like(acc_ref)

    act = _dot_nt(u_ref[...], x_ref[...])
    g = _gelu(act)
    n_i1 = te // N_KEYS
    parts = []
    for a in range(n_i1):
        i1 = j * n_i1 + a
        w = None
        for h in range(PEER_HEADS):
            s1r = s1_ref[h, pl.ds(i1, 1), :]
            e1r = e1_ref[h, pl.ds(i1, 1), :]
            sel = jnp.where(s1r + s2_ref[h] >= thr_ref[h:h + 1, :], e1r * e2_ref[h], 0.0)
            w = sel if w is None else w + sel
        parts.append((w * g[a * N_KEYS:(a + 1) * N_KEYS, :]).astype(bf16))
    coef = jnp.concatenate(parts, axis=0)
    acc_ref[...] += _dot(vt_ref[...], coef)

    @pl.when(j == pl.num_programs(1) - 1)
    def _():
        o_ref[...] = acc_ref[...].T


def _peer(xn, u_bf, vt_bf, s1, e1, s2, e2, thr, tm, te):
    n = xn.shape[0]
    n_exp = u_bf.shape[0]
    tok = pl.BlockSpec((PEER_HEADS, N_KEYS, tm), lambda i, j: (0, 0, i))
    return pl.pallas_call(
        functools.partial(_peer_kernel, te=te),
        grid=(n // tm, n_exp // te),
        in_specs=[
            pl.BlockSpec((tm, D_MODEL), lambda i, j: (i, 0)),
            pl.BlockSpec((te, D_MODEL), lambda i, j: (j, 0)),
            pl.BlockSpec((D_MODEL, te), lambda i, j: (0, j)),
            tok, tok, tok, tok,
            pl.BlockSpec((PEER_HEADS, tm), lambda i, j: (0, i)),
        ],
        out_specs=pl.BlockSpec((tm, D_MODEL), lambda i, j: (i, 0)),
        out_shape=jax.ShapeDtypeStruct((n, D_MODEL), f32),
        scratch_shapes=[pltpu.VMEM((D_MODEL, tm), f32)],
        compiler_params=_params("parallel", "arbitrary"),
        name="peer_dense",
    )(xn, u_bf, vt_bf, s1, e1, s2, e2, thr)


def _final_kernel(h_ref, p_ref, g_ref, o_ref):
    y = h_ref[...] + p_ref[...]
    o_ref[...] = y * lax.rsqrt(jnp.mean(y * y, axis=-1, keepdims=True) + RMS_EPS) * g_ref[...]


def _final(h, p, g, tm):
    n, d = h.shape
    blk = pl.BlockSpec((tm, d), lambda i: (i, 0))
    return pl.pallas_call(
        _final_kernel,
        grid=(n // tm,),
        in_specs=[blk, blk, pl.BlockSpec((1, d), lambda i: (0, 0))],
        out_specs=blk,
        out_shape=jax.ShapeDtypeStruct((n, d), f32),
        compiler_params=_params("parallel"),
        name="final_norm",
    )(h, p, g.reshape(1, d))


def _pad_rows(w, rows, at=0):
    out = jnp.zeros((rows, w.shape[1]), w.dtype)
    return lax.dynamic_update_slice(out, w, (at, 0))


def _prep_weights(w_in, ws, bs, mu, w0, w_up, a0, a_up, g_up, k_k, k_a):
    pad_c = W_RWKV_PAD - W_RWKV_IN
    w_cat = jnp.concatenate([w_in[:, 2 * W_A:], jnp.zeros((D_MODEL, pad_c), f32), w_in[:, :2 * W_A]], axis=1).astype(bf16)
    mu_p = jnp.concatenate([mu, jnp.zeros((pad_c,), f32)]).reshape(1, W_RWKV_PAD)
    tril = jnp.tril(jnp.ones((CHUNK_A, CHUNK_A), bool))
    ws_prompt = jnp.where(tril[None], ws, 0.0).astype(bf16)
    bs_prompt = jnp.broadcast_to(bs[:, :, None], (H_A, CHUNK_A, DA))
    n_tok = 4
    e = jnp.zeros((H_A, SAMPLE_ROWS, SAMPLE_ROWS), f32).at[:, 1:1 + n_tok, 1:1 + n_tok].set(
        jnp.where(tril[:n_tok, :n_tok][None], ws[:, :n_tok, :n_tok], 0.0))
    reps = CHUNK_A // SAMPLE_ROWS
    ws_sample = jnp.einsum("pq,hab->hpaqb", jnp.eye(reps, dtype=f32), e).reshape(H_A, CHUNK_A, CHUNK_A).astype(bf16)
    b8 = jnp.zeros((H_A, SAMPLE_ROWS), f32).at[:, 1:1 + n_tok].set(bs[:, :n_tok])
    bs_sample = jnp.broadcast_to(jnp.tile(b8, (1, reps))[:, :, None], (H_A, CHUNK_A, DA))
    rw = {
        "mu": mu_p, "w0": w0.reshape(1, W_B), "a0": a0.reshape(1, W_B),
        "w_up": _pad_rows(w_up, 128, 0), "a_up": _pad_rows(a_up, 128, LORA_W), "g_up": _pad_rows(g_up, 256, 0),
        "k_k": k_k.reshape(1, W_B), "k_a": k_a.reshape(1, W_B),
    }
    return w_cat, (ws_prompt, bs_prompt), (ws_sample, bs_sample), rw


def _mix(xn_rows, s0, n_seq, seq_len, v_lo, v_hi, C, bb, tm_mm, tm_tok, w_cat, sg, rw, ln_v_g, ln_v_b, r_k, gn_g, gn_b):
    proj = _matmul(xn_rows, w_cat, tm_mm, 512)
    ya, vn = _sgu(proj, ln_v_g, ln_v_b, *sg)
    r, ld, kp, v, kkr, br, g = _rwkv_prep(proj, rw, tm_tok, seq_len, v_lo, v_hi)
    rh, y0, mg = _rwkv_chunk(r, ld, kp, v, kkr, br, C)
    y, s_new = _rwkv_scan(s0, rh, y0, mg, n_seq, C, bb)
    yb = _rwkv_post(y, r, kp, v, g, r_k, gn_g, gn_b, tm_tok)
    return ya, yb, vn, s_new


def _channel_mix(x, ya, yb, w_out_bf, norm2_g, w_q_bf, sub_keys, u_bf, vt_bf, final_g, tm_peer, te):
    h, xn2 = _out_proj(ya, yb, w_out_bf, x, norm2_g, 256)
    s1, e1, s2, e2, thr = _router(xn2, w_q_bf, sub_keys, 256)
    p = _peer(xn2, u_bf, vt_bf, s1, e1, s2, e2, thr, tm_peer, te)
    return _final(h, p, final_g, 512)


def kernel(x_prompt, x_sample, state_wkv, state_shift, norm1_g, w_in, ln_v_g, ln_v_b, ws, bs, mu, w0, w_up, a0, a_up, g_up, k_k, k_a, r_k, gn_g, gn_b, w_out, norm2_g, w_q, sub_keys, u_tab, v_tab, final_g):
    assert norm1_g.shape[0] == 1, "single trunk layer"
    bp, t_p, d = x_prompt.shape
    bs_n, t_s, _ = x_sample.shape
    w_cat, sg_p, sg_s, rw = _prep_weights(w_in[0], ws[0], bs[0], mu[0], w0[0], w_up[0], a0[0], a_up[0], g_up[0], k_k[0], k_a[0])
    w_out_bf = w_out[0].astype(bf16)
    w_q_bf = w_q[0].astype(bf16)
    u_bf = u_tab[0].astype(bf16)
    vt_bf = v_tab[0].astype(bf16).T
    shared = (ln_v_g[0], ln_v_b[0], r_k[0].reshape(W_B), gn_g[0], gn_b[0])
    cm = (w_out_bf, norm2_g[0], w_q_bf, sub_keys[0], u_bf, vt_bf, final_g)

    xp = x_prompt.reshape(bp * t_p, d)
    xn_p = _rmsnorm(xp, norm1_g[0], bf16, 512)
    s0_p = jnp.zeros((bp, H_B, HEAD_B, HEAD_B), f32)
    ya, yb, vn_p, wkv_p = _mix(xn_p, s0_p, bp, t_p, 0, t_p, 64, bp, 1024, 256, w_cat, sg_p, rw, *shared)
    y_p = _channel_mix(xp, ya, yb, *cm, 512, 512)

    xs = x_sample.reshape(bs_n * t_s, d)
    xn_s = _rmsnorm(xs, norm1_g[0], bf16, 512).reshape(bs_n, t_s, d)
    rows = jnp.concatenate([state_shift[0].astype(bf16)[:, None], xn_s,
                            jnp.zeros((bs_n, SAMPLE_ROWS - 1 - t_s, d), bf16)], axis=1).reshape(bs_n * SAMPLE_ROWS, d)
    ya8, yb8, vn8, wkv_s = _mix(rows, state_wkv[0], bs_n, SAMPLE_ROWS, 1, 1 + t_s, SAMPLE_ROWS, 8, 512, 256,
                                w_cat, sg_s, rw, *shared)
    tok = lambda a: a.reshape(bs_n, SAMPLE_ROWS, -1)[:, 1:1 + t_s].reshape(bs_n * t_s, -1)
    y_s = _channel_mix(xs, tok(ya8), tok(yb8), *cm, 512, 512)

    last = jnp.concatenate([x_prompt[:, -1], x_sample[:, -1]], axis=0)
    n_last = last.shape[0]
    pad = (-n_last) % 8
    last = jnp.concatenate([last, jnp.zeros((pad, d), f32)], axis=0)
    xn_last = _rmsnorm(last, norm1_g[0], f32, n_last + pad)

    chunk_rows = t_p - ((t_p - 1) // CHUNK_A) * CHUNK_A
    return (
        y_p.reshape(bp, t_p, d),
        y_s.reshape(bs_n, t_s, d),
        wkv_p[None],
        xn_last[:bp][None],
        vn_p.reshape(bp, t_p, W_A)[:, t_p - chunk_rows:][None],
        wkv_s[None],
        xn_last[bp:bp + bs_n][None],
        vn8.reshape(bs_n, SAMPLE_ROWS, W_A)[:, 1:1 + t_s][None],
    )
```
